```python
import jax
import jax.numpy as jnp
from jax import lax
import numpy as np

D_MODEL = 1024
BATCH = 32
SEQ = 256
DEPTH = 2
DEC_BATCH = 2
DEC_SEQ = 4096
PAST_LEN = 256

GRID_W = 64
EPS = 1e-6
ADA_CHUNKS = 6
GLA_W = D_MODEL // 4
GLA_DK = 64
GLA_DV = 64
GLA_HEADS = GLA_W // GLA_DV
GLA_RANK = 16
GLA_NORMALIZER = 16.0
GLA_CHUNK = 64
CONV_W = D_MODEL // 4
CONV_GROUPS = 4
CONV_K = 31
HEAD_DIM = 64
ATT_W = D_MODEL // 2
ATT_HEADS = ATT_W // HEAD_DIM
ATT_KV_HEADS = 2
ATT_GROUP = ATT_HEADS // ATT_KV_HEADS
Q_BLOCK = 128
ROPE_THETA = 10000.0
MIX_W = GLA_W + CONV_W + ATT_W
IN_SIZES = (GLA_HEADS * GLA_DK, GLA_HEADS * GLA_DK, GLA_W, GLA_W, GLA_RANK, GLA_RANK,
            CONV_W, CONV_W, ATT_W, ATT_KV_HEADS * HEAD_DIM, ATT_KV_HEADS * HEAD_DIM)
IN_COLS = sum(IN_SIZES)
PEER_HEADS = 8
PEER_NKEYS = 128
PEER_EXPERTS = PEER_NKEYS * PEER_NKEYS
PEER_DKEY = 128
PEER_TOPK = 16
TOKEN_BLOCK = 128

kernel_name = 'hymba_gla_conformer_gqa_peer_diffusion_step'


def rms_norm(x, g):
    xf = x.astype(jnp.float32)
    y = xf * lax.rsqrt(jnp.mean(xf * xf, axis=-1, keepdims=True) + EPS)
    return (y * g.astype(jnp.float32)).astype(x.dtype)


def group_layer_norm(u, g, b):
    bsz, t, ch = u.shape
    uf = u.astype(jnp.float32).reshape(bsz, t, CONV_GROUPS, ch // CONV_GROUPS)
    d = uf - jnp.mean(uf, axis=-1, keepdims=True)
    y = (d * lax.rsqrt(jnp.mean(d * d, axis=-1, keepdims=True) + EPS)).reshape(bsz, t, ch)
    return (y * g.astype(jnp.float32) + b.astype(jnp.float32)).astype(u.dtype)


def rope_axis(x, pos):
    m = x.shape[-1] // 2
    freqs = ROPE_THETA ** (-jnp.arange(m, dtype=jnp.float32) / m)
    ang = pos.astype(jnp.float32)[:, None] * freqs[None, :]
    cos = jnp.cos(ang)[None, :, None, :]
    sin = jnp.sin(ang)[None, :, None, :]
    xf = x.astype(jnp.float32)
    x1, x2 = xf[..., :m], xf[..., m:]
    return jnp.concatenate([x1 * cos - x2 * sin, x1 * sin + x2 * cos], axis=-1).astype(x.dtype)


def rope_2d(x):
    t = x.shape[1]
    rows = t // GRID_W
    pos = jnp.arange(rows * GRID_W)
    half = x.shape[-1] // 2
    return jnp.concatenate([rope_axis(x[..., :half], pos // GRID_W),
                            rope_axis(x[..., half:], pos % GRID_W)], axis=-1)


def attend(q, k, v):
    bsz, t = q.shape[:2]
    nb = t // Q_BLOCK
    qb = q.reshape(bsz, nb, Q_BLOCK, ATT_KV_HEADS, ATT_GROUP, HEAD_DIM).transpose(1, 0, 2, 3, 4, 5)

    def block(qi):
        s = jnp.einsum('bqkgd,bskd->bkgqs', qi, k).astype(jnp.float32)
        p = jax.nn.softmax(s, axis=-1).astype(v.dtype)
        return jnp.einsum('bkgqs,bskd->bqkgd', p, v)

    o = lax.map(block, qb)
    return o.transpose(1, 0, 2, 3, 4, 5).reshape(bsz, t, ATT_W)


def gla_chunk_scan(q, k, v, log_a, s0):
    bsz, t, nh, _ = q.shape
    n = t // GLA_CHUNK

    def chunks(z):
        return z.astype(jnp.float32).reshape(bsz, n, GLA_CHUNK, nh, z.shape[-1]).transpose(1, 0, 3, 2, 4)

    qc, kc, vc = chunks(q), chunks(k), chunks(v)
    bc = jnp.cumsum(chunks(log_a), axis=-2)
    lower = jnp.tril(jnp.ones((GLA_CHUNK, GLA_CHUNK), dtype=bool))

    def step(state, inp):
        qi, ki, vi, bi = inp
        q_dec = qi * jnp.exp(bi)
        k_dec = ki * jnp.exp(-bi)
        a = jnp.where(lower, jnp.einsum('bhid,bhjd->bhij', q_dec, k_dec), 0.0)
        o = jnp.einsum('bhij,bhje->bhie', a, vi) + jnp.einsum('bhid,bhde->bhie', q_dec, state)
        b_last = bi[:, :, -1, :]
        state = state * jnp.exp(b_last)[..., None] + jnp.einsum(
            'bhjd,bhje->bhde', ki * jnp.exp(b_last[:, :, None, :] - bi), vi)
        return state, o

    state, o = lax.scan(step, s0.astype(jnp.float32), (qc, kc, vc, bc))
    o = o.transpose(1, 0, 3, 2, 4).reshape(bsz, t, nh, v.shape[-1])
    return o.astype(q.dtype), state


def dwconv(u, w, b):
    ch = u.shape[-1]
    y = lax.conv_general_dilated(u, w[:, None, :].astype(u.dtype), (1,), [(CONV_K // 2, CONV_K // 2)],
                                 dimension_numbers=('NWC', 'WIO', 'NWC'), feature_group_count=ch)
    return y + b


def peer(h, wq, sub_keys, peer_u, peer_v):
    bsz, t, d = h.shape
    flat = h.reshape(-1, TOKEN_BLOCK, d)

    def block(xb):
        n = xb.shape[0]
        qry = (xb @ wq).reshape(n, PEER_HEADS, 2, PEER_DKEY // 2)
        s1 = jnp.einsum('nhd,hkd->nhk', qry[:, :, 0], sub_keys[:, 0]).astype(jnp.float32)
        s2 = jnp.einsum('nhd,hkd->nhk', qry[:, :, 1], sub_keys[:, 1]).astype(jnp.float32)
        v1, i1 = lax.top_k(s1, PEER_TOPK)
        v2, i2 = lax.top_k(s2, PEER_TOPK)
        cand = (v1[..., :, None] + v2[..., None, :]).reshape(n, PEER_HEADS, PEER_TOPK * PEER_TOPK)
        cidx = (i1[..., :, None] * PEER_NKEYS + i2[..., None, :]).reshape(n, PEER_HEADS, PEER_TOPK * PEER_TOPK)
        sc, pos = lax.top_k(cand, PEER_TOPK)
        idx = jnp.take_along_axis(cidx, pos, axis=-1)
        g = jax.nn.softmax(sc, axis=-1)
        act = jax.nn.gelu(jnp.einsum('nd,nhkd->nhk', xb, peer_u[idx]).astype(jnp.float32), approximate=False)
        w = (g * act).astype(xb.dtype)
        return jnp.einsum('nhk,nhkd->nd', w, peer_v[idx])

    return lax.map(block, flat).reshape(bsz, t, d)


def trunk_layer(x, cvec, p, l, ctx):
    bsz, t, _ = x.shape
    m = jax.nn.silu(cvec) @ p['ada_w'][l] + p['ada_b'][l]
    m = m.reshape(m.shape[:-1] + (1, ADA_CHUNKS, D_MODEL))
    shift1, scale1, gate1, shift2, scale2, gate2 = [m[..., i, :] for i in range(ADA_CHUNKS)]

    h = rms_norm(x, p['norm1_g'][l]) * (1 + scale1) + shift1
    proj = h @ p['w_in'][l]
    gq, gk, gv, gg, gaf, gab, cval, cgate, aq, ak, av = jnp.split(
        proj, np.cumsum(IN_SIZES)[:-1].tolist(), axis=-1)

    def heads(z, dh):
        return z.reshape(bsz, t, -1, dh)

    la_f = jax.nn.log_sigmoid((gaf @ p['gla_a_w_f'][l] + p['gla_a_b_f'][l]).astype(jnp.float32)) / GLA_NORMALIZER
    la_b = jax.nn.log_sigmoid((gab @ p['gla_a_w_b'][l] + p['gla_a_b_b'][l]).astype(jnp.float32)) / GLA_NORMALIZER
    qg = heads(gq, GLA_DK) * (GLA_DK ** -0.5)
    kg = heads(gk, GLA_DK)
    vg = heads(gv, GLA_DV)
    if ctx is None:
        s0_f = jnp.zeros((bsz, GLA_HEADS, GLA_DK, GLA_DV), jnp.float32)
        s0_b = s0_f
    else:
        s0_f, s0_b = ctx[2], ctx[3]
    o_f, s_f = gla_chunk_scan(qg, kg, vg, heads(la_f, GLA_DK), s0_f)
    o_b, s_b = gla_chunk_scan(jnp.flip(qg, 1), jnp.flip(kg, 1), jnp.flip(vg, 1),
                              jnp.flip(heads(la_b, GLA_DK), 1), s0_b)
    o_gla = rms_norm(o_f + jnp.flip(o_b, 1), p['gla_norm_g'][l]) * jax.nn.silu(heads(gg, GLA_DV))
    o_gla = o_gla.reshape(bsz, t, GLA_W)

    u = cval * jax.nn.sigmoid(cgate)
    u = dwconv(u, p['conv_w'][l], p['conv_b'][l])
    u = jax.nn.silu(group_layer_norm(u, p['conv_norm_g'][l], p['conv_norm_b'][l]))

    qa = rms_norm(heads(aq, HEAD_DIM), p['q_norm_g'][l])
    ka = rms_norm(heads(ak, HEAD_DIM), p['k_norm_g'][l])
    va = heads(av, HEAD_DIM)
    if ctx is None:
        keys, vals = ka, va
    else:
        qa = rope_2d(qa)
        keys = jnp.concatenate([ctx[0], rope_2d(ka)], axis=1)
        vals = jnp.concatenate([ctx[1], va], axis=1)
    qa = (qa * (HEAD_DIM ** -0.5)).reshape(bsz, t, ATT_KV_HEADS, ATT_GROUP, HEAD_DIM)
    o_att = attend(qa, keys, vals)

    mix = jnp.concatenate([o_gla, u, o_att], axis=-1) @ p['w_out'][l]
    x = x + gate1 * mix
    h2 = rms_norm(x, p['norm2_g'][l]) * (1 + scale2) + shift2
    x = x + gate2 * peer(h2, p['peer_wq'][l], p['peer_keys'][l], p['peer_u'][l], p['peer_v'][l])
    return x, ka, va, s_f.astype(x.dtype), s_b.astype(x.dtype)


def setup_inputs(seed: int = 0) -> dict:
    key = jax.random.key(seed)
    ks = jax.random.split(key, 32)

    def nrm(k, shape, s):
        return jax.random.normal(k, shape, jnp.float32) * s

    def gain(k, shape):
        return 1.0 + 0.02 * jax.random.normal(k, shape, jnp.float32)

    return {
        'x_prompt': nrm(ks[0], (BATCH, SEQ, D_MODEL), 1.0),
        'x_sample': nrm(ks[1], (DEC_BATCH, DEC_SEQ, D_MODEL), 1.0),
        'c': nrm(ks[2], (DEC_BATCH, D_MODEL), 1.0),
        'cache_k': nrm(ks[3], (DEC_BATCH, DEPTH, PAST_LEN, ATT_KV_HEADS, HEAD_DIM), 1.0),
        'cache_v': nrm(ks[4], (DEC_BATCH, DEPTH, PAST_LEN, ATT_KV_HEADS, HEAD_DIM), 1.0),
        'state_gla': nrm(ks[5], (DEC_BATCH, DEPTH, 2, GLA_HEADS, GLA_DK, GLA_DV), 1.0),
        'c_ctx': nrm(ks[6], (D_MODEL,), 1.0),
        'ada_w': nrm(ks[7], (DEPTH, D_MODEL, ADA_CHUNKS * D_MODEL), 0.5 * D_MODEL ** -0.5),
        'ada_b': nrm(ks[8], (DEPTH, ADA_CHUNKS * D_MODEL), 0.02),
        'norm1_g': gain(ks[9], (DEPTH, D_MODEL)),
        'norm2_g': gain(ks[10], (DEPTH, D_MODEL)),
        'w_in': nrm(ks[11], (DEPTH, D_MODEL, IN_COLS), D_MODEL ** -0.5),
        'gla_a_w_f': nrm(ks[12], (DEPTH, GLA_RANK, GLA_HEADS * GLA_DK), GLA_RANK ** -0.5),
        'gla_a_b_f': nrm(ks[13], (DEPTH, GLA_HEADS * GLA_DK), 0.02),
        'gla_a_w_b': nrm(ks[14], (DEPTH, GLA_RANK, GLA_HEADS * GLA_DK), GLA_RANK ** -0.5),
        'gla_a_b_b': nrm(ks[15], (DEPTH, GLA_HEADS * GLA_DK), 0.02),
        'gla_norm_g': gain(ks[16], (DEPTH, GLA_HEADS, GLA_DV)),
        'conv_w': nrm(ks[17], (DEPTH, CONV_K, CONV_W), CONV_K ** -0.5),
        'conv_b': nrm(ks[18], (DEPTH, CONV_W), 0.02),
        'conv_norm_g': gain(ks[19], (DEPTH, CONV_W)),
        'conv_norm_b': nrm(ks[20], (DEPTH, CONV_W), 0.02),
        'q_norm_g': gain(ks[21], (DEPTH, HEAD_DIM)),
        'k_norm_g': gain(ks[22], (DEPTH, HEAD_DIM)),
        'w_out': nrm(ks[23], (DEPTH, MIX_W, D_MODEL), MIX_W ** -0.5),
        'peer_wq': nrm(ks[24], (DEPTH, D_MODEL, PEER_HEADS * PEER_DKEY), D_MODEL ** -0.5),
        'peer_keys': nrm(ks[25], (DEPTH, PEER_HEADS, 2, PEER_NKEYS, PEER_DKEY // 2), (PEER_DKEY // 2) ** -0.5),
        'peer_u': nrm(ks[26], (DEPTH, PEER_EXPERTS, D_MODEL), D_MODEL ** -0.5),
        'peer_v': nrm(ks[27], (DEPTH, PEER_EXPERTS, D_MODEL), 0.5),
    }


def reference(x_prompt, x_sample, c, cache_k, cache_v, state_gla, c_ctx, ada_w, ada_b, norm1_g, norm2_g,
              w_in, gla_a_w_f, gla_a_b_f, gla_a_w_b, gla_a_b_b, gla_norm_g, conv_w, conv_b,
              conv_norm_g, conv_norm_b, q_norm_g, k_norm_g, w_out, peer_wq, peer_keys, peer_u, peer_v):
    p = {'ada_w': ada_w, 'ada_b': ada_b, 'norm1_g': norm1_g, 'norm2_g': norm2_g, 'w_in': w_in,
         'gla_a_w_f': gla_a_w_f, 'gla_a_b_f': gla_a_b_f, 'gla_a_w_b': gla_a_w_b, 'gla_a_b_b': gla_a_b_b,
         'gla_norm_g': gla_norm_g, 'conv_w': conv_w, 'conv_b': conv_b, 'conv_norm_g': conv_norm_g,
         'conv_norm_b': conv_norm_b, 'q_norm_g': q_norm_g, 'k_norm_g': k_norm_g, 'w_out': w_out,
         'peer_wq': peer_wq, 'peer_keys': peer_keys, 'peer_u': peer_u, 'peer_v': peer_v}

    x = x_prompt
    ks_, vs_, sts_ = [], [], []
    for l in range(DEPTH):
        x, k_l, v_l, sf_l, sb_l = trunk_layer(x, c_ctx, p, l, None)
        ks_.append(k_l)
        vs_.append(v_l)
        sts_.append(jnp.stack([sf_l, sb_l], axis=1))
    y_prompt = x
    new_cache_k = jnp.stack(ks_, axis=1)
    new_cache_v = jnp.stack(vs_, axis=1)
    new_state_gla = jnp.stack(sts_, axis=1)

    x = x_sample
    for l in range(DEPTH):
        ctx = (cache_k[:, l], cache_v[:, l], state_gla[:, l, 0], state_gla[:, l, 1])
        x, _, _, _, _ = trunk_layer(x, c, p, l, ctx)
    y_sample = x

    return (y_prompt, y_sample, new_cache_k, new_cache_v, new_state_gla)
```

```python
import functools

import jax
import jax.numpy as jnp
from jax import lax
from jax.experimental import pallas as pl
from jax.experimental.pallas import tpu as pltpu

F32 = jnp.float32
BF16 = jnp.bfloat16

EPS = 1e-6
ADA_CHUNKS = 6
GRID_W = 64
ROPE_THETA = 10000.0
GLA_HEADS = 4
GLA_DK = 64
GLA_RANK = 16
GLA_NORMALIZER = 16.0
GLA_CHUNK = 64
CONV_GROUPS = 4
CONV_K = 31
HEAD_DIM = 64
ATT_KV_HEADS = 2
PEER_HEADS = 8
PEER_TOPK = 16

LANES = 128
SUBLANES = 8
VMEM_LIMIT = 56 * 1024 * 1024


def _cparams(sem):
    return pltpu.CompilerParams(dimension_semantics=sem, vmem_limit_bytes=VMEM_LIMIT)


def _split_dot(x, m_bf16, terms):
    acc = None
    rem = x
    for _ in range(terms):
        piece = rem.astype(BF16)
        part = jnp.dot(piece, m_bf16, preferred_element_type=F32)
        acc = part if acc is None else acc + part
        rem = rem - piece.astype(F32)
    return acc


def _seg_matrix(width, seg, value):
    r = lax.broadcasted_iota(jnp.int32, (width, width), 0) // seg
    c = lax.broadcasted_iota(jnp.int32, (width, width), 1) // seg
    return jnp.where(r == c, value, 0.0).astype(BF16)


def _sigmoid(x):
    return 1.0 / (1.0 + jnp.exp(-x))


def _silu(x):
    return x * _sigmoid(x)


def _mod_kernel(c_ref, w_ref, b_ref, o_ref):
    c = c_ref[...]
    o_ref[...] = jnp.dot(_silu(c), w_ref[...], preferred_element_type=F32) + b_ref[...]


def _modulation(cvecs, ada_w_l, ada_b_l):
    d, n = ada_w_l.shape
    tn = 1024
    return pl.pallas_call(
        _mod_kernel,
        grid=(n // tn,),
        in_specs=[pl.BlockSpec((8, d), lambda j: (0, 0)),
                  pl.BlockSpec((d, tn), lambda j: (0, j)),
                  pl.BlockSpec((1, tn), lambda j: (0, j))],
        out_specs=pl.BlockSpec((8, tn), lambda j: (0, j)),
        out_shape=jax.ShapeDtypeStruct((8, n), F32),
        compiler_params=_cparams(("parallel",)),
        name="modulation",
    )(cvecs, ada_w_l, ada_b_l.reshape(1, n))


def _head_rms(x, seg_mat, gain):
    ms = _split_dot(x * x, seg_mat, 2)
    return x * lax.rsqrt(ms + EPS) * gain


def _rope(x, cos, sin_signed):
    w = x.shape[-1]
    lane = lax.broadcasted_iota(jnp.int32, x.shape, 1)
    partner = jnp.where((lane % 32) < 16, pltpu.roll(x, w - 16, 1), pltpu.roll(x, 16, 1))
    return x * cos + partner * sin_signed


def _pre_kernel(rope, x_ref, mod_ref, n1g_ref, win_ref, aw_ref, ab_ref, qg_ref, kg_ref, *rest):
    if rope:
        cq_ref, sq_ref, ck_ref, sk_ref, gla_ref, la_ref, u_ref, q_ref, k_ref, v_ref = rest
    else:
        gla_ref, la_ref, u_ref, q_ref, k_ref, v_ref = rest
    x = x_ref[...]
    shift = mod_ref[0, 0:1, :]
    scale = mod_ref[0, 1:2, :]
    h = x * lax.rsqrt(jnp.mean(x * x, axis=-1, keepdims=True) + EPS) * n1g_ref[...]
    h = h * (1.0 + scale) + shift
    proj = jnp.dot(h, win_ref[...], preferred_element_type=F32)
    gla_ref[:, 0:256] = proj[:, 0:256] * (GLA_DK ** -0.5)
    gla_ref[:, 256:1024] = proj[:, 256:1024]
    u_ref[...] = proj[:, 1024:1280] * _sigmoid(proj[:, 1280:1536])
    la_pre = jnp.dot(proj[:, 2304:2432], aw_ref[...], preferred_element_type=F32) + ab_ref[...]
    la_ref[...] = (jnp.minimum(la_pre, 0.0) - jnp.log1p(jnp.exp(-jnp.abs(la_pre)))) * (1.0 / GLA_NORMALIZER)
    q = _head_rms(proj[:, 1536:2048], _seg_matrix(512, HEAD_DIM, 1.0 / HEAD_DIM), qg_ref[...])
    k = _head_rms(proj[:, 2048:2176], _seg_matrix(128, HEAD_DIM, 1.0 / HEAD_DIM), kg_ref[...])
    if rope:
        q = _rope(q, cq_ref[...], sq_ref[...])
        k = _rope(k, ck_ref[...], sk_ref[...])
    q_ref[...] = q * (HEAD_DIM ** -0.5)
    k_ref[...] = k
    v_ref[...] = proj[:, 2176:2304]


def _pre(x, mod, n1g, win_r, aw, ab, qg, kg, rope_tabs, tokens_per_mod, seq):
    n, d = x.shape
    tm = 256
    cols = win_r.shape[1]
    rope = rope_tabs is not None
    tiles_per_mod = tokens_per_mod // tm
    tiles_per_seq = seq // tm
    const = lambda i: (0, 0)
    row = lambda i: (i, 0)
    in_specs = [pl.BlockSpec((tm, d), row),
                pl.BlockSpec((1, ADA_CHUNKS, d), lambda i: (i // tiles_per_mod, 0, 0)),
                pl.BlockSpec((1, d), const),
                pl.BlockSpec((d, cols), const),
                pl.BlockSpec((LANES, 512), const),
                pl.BlockSpec((1, 512), const),
                pl.BlockSpec((1, 512), const),
                pl.BlockSpec((1, 128), const)]
    args = [x, mod, n1g, win_r, aw, ab, qg, kg]
    if rope:
        pos = lambda i: (i % tiles_per_seq, 0)
        in_specs += [pl.BlockSpec((tm, 512), pos), pl.BlockSpec((tm, 512), pos),
                     pl.BlockSpec((tm, 128), pos), pl.BlockSpec((tm, 128), pos)]
        args += list(rope_tabs)
    widths = (1024, 512, 256, 512, 128, 128)
    return pl.pallas_call(
        functools.partial(_pre_kernel, rope),
        grid=(n // tm,),
        in_specs=in_specs,
        out_specs=[pl.BlockSpec((tm, w), row) for w in widths],
        out_shape=[jax.ShapeDtypeStruct((n, w), F32) for w in widths],
        compiler_params=_cparams(("parallel",)),
        name="pre_rope" if rope else "pre",
    )(*args)


def _gla_kernel(reverse, nchunk, q_ref, k_ref, v_ref, la_ref, s0_ref, o_ref, sT_out_ref, st_ref):
    j = pl.program_id(1)

    @pl.when(j == 0)
    def _():
        st_ref[...] = s0_ref[0]

    r = lax.broadcasted_iota(jnp.int32, (GLA_CHUNK, GLA_CHUNK), 0)
    c = lax.broadcasted_iota(jnp.int32, (GLA_CHUNK, GLA_CHUNK), 1)
    keep = (c >= r) if reverse else (c <= r)
    tri = jnp.where(keep, 1.0, 0.0).astype(BF16)
    order = range(nchunk - 1, -1, -1) if reverse else range(nchunk)
    for ci in order:
        rows = slice(ci * GLA_CHUNK, (ci + 1) * GLA_CHUNK)
        la = la_ref[rows, :]
        bcum = None
        rem = la
        for _ in range(3):
            piece = rem.astype(BF16)
            part = jnp.dot(tri, piece, preferred_element_type=F32)
            bcum = part if bcum is None else bcum + part
            rem = rem - piece.astype(F32)
        for h in range(GLA_HEADS):
            cs = slice(h * GLA_DK, (h + 1) * GLA_DK)
            qh = q_ref[rows, cs]
            kh = k_ref[rows, cs]
            vh = v_ref[rows, cs]
            bh = bcum[:, cs]
            b_end = bh[0:1, :] if reverse else bh[GLA_CHUNK - 1:GLA_CHUNK, :]
            q_dec = qh * jnp.exp(bh)
            k_dec = kh * jnp.exp(-bh)
            a = lax.dot_general(q_dec, k_dec, (((1,), (1,)), ((), ())), preferred_element_type=F32)
            a = jnp.where(keep, a, 0.0)
            st = st_ref[h]
            o = jnp.dot(a, vh, preferred_element_type=F32) + lax.dot_general(
                q_dec, st, (((1,), (1,)), ((), ())), preferred_element_type=F32)
            o_ref[rows, cs] = o
            k_tail = kh * jnp.exp(b_end - bh)
            st_ref[h] = st * jnp.exp(b_end) + lax.dot_general(
                vh, k_tail, (((0,), (0,)), ((), ())), preferred_element_type=F32)

    @pl.when(j == pl.num_programs(1) - 1)
    def _():
        sT_out_ref[0] = st_ref[...]


def _gla_scan(gla, la, s0T, nb, seq, reverse):
    n = gla.shape[0]
    tb = 256
    nt = seq // tb
    w = GLA_HEADS * GLA_DK

    def rowblk(b, j):
        return b * nt + (nt - 1 - j if reverse else j)

    la_col = 1 if reverse else 0
    st_shape = (GLA_HEADS, GLA_DK, GLA_DK)
    return pl.pallas_call(
        functools.partial(_gla_kernel, reverse, tb // GLA_CHUNK),
        grid=(nb, nt),
        in_specs=[pl.BlockSpec((tb, w), lambda b, j: (rowblk(b, j), 0)),
                  pl.BlockSpec((tb, w), lambda b, j: (rowblk(b, j), 1)),
                  pl.BlockSpec((tb, w), lambda b, j: (rowblk(b, j), 2)),
                  pl.BlockSpec((tb, w), lambda b, j: (rowblk(b, j), la_col)),
                  pl.BlockSpec((1,) + st_shape, lambda b, j: (b, 0, 0, 0))],
        out_specs=[pl.BlockSpec((tb, w), lambda b, j: (rowblk(b, j), 0)),
                   pl.BlockSpec((1,) + st_shape, lambda b, j: (b, 0, 0, 0))],
        out_shape=[jax.ShapeDtypeStruct((n, w), F32),
                   jax.ShapeDtypeStruct((nb,) + st_shape, F32)],
        scratch_shapes=[pltpu.VMEM(st_shape, F32)],
        compiler_params=_cparams(("parallel", "arbitrary")),
        name="gla_bwd" if reverse else "gla_fwd",
    )(gla, gla, gla, la, s0T)


CONV_HALO = 16


def _conv_kernel(nt, prev_ref, cur_ref, next_ref, w_ref, b_ref, ng_ref, nb_ref, o_ref, pad_ref):
    j = pl.program_id(0) % nt
    tb = cur_ref.shape[0]
    pad_ref[0:CONV_HALO, :] = jnp.where(j > 0, prev_ref[...], 0.0)
    pad_ref[CONV_HALO:CONV_HALO + tb, :] = cur_ref[...]
    pad_ref[CONV_HALO + tb:, :] = jnp.where(j < nt - 1, next_ref[...], 0.0)
    base = CONV_HALO - CONV_K // 2
    acc = None
    for kk in range(CONV_K):
        term = pad_ref[base + kk:base + kk + tb, :] * w_ref[kk:kk + 1, :]
        acc = term if acc is None else acc + term
    y = acc + b_ref[...]
    ch = y.shape[-1]
    seg = _seg_matrix(ch, ch // CONV_GROUPS, float(CONV_GROUPS) / ch)
    dlt = y - _split_dot(y, seg, 3)
    var = _split_dot(dlt * dlt, seg, 2)
    z = dlt * lax.rsqrt(var + EPS) * ng_ref[...] + nb_ref[...]
    o_ref[...] = _silu(z)


def _conv_branch(u, conv_w_l, conv_b_l, ng, nb_, seq):
    n, ch = u.shape
    tb = 256
    nt = seq // tb
    hb = tb // CONV_HALO
    nhalo = n // CONV_HALO
    const = lambda i: (0, 0)
    return pl.pallas_call(
        functools.partial(_conv_kernel, nt),
        grid=(n // tb,),
        in_specs=[pl.BlockSpec((CONV_HALO, ch), lambda i: (jnp.maximum(i * hb - 1, 0), 0)),
                  pl.BlockSpec((tb, ch), lambda i: (i, 0)),
                  pl.BlockSpec((CONV_HALO, ch), lambda i: (jnp.minimum((i + 1) * hb, nhalo - 1), 0)),
                  pl.BlockSpec((CONV_K, ch), const),
                  pl.BlockSpec((1, ch), const), pl.BlockSpec((1, ch), const), pl.BlockSpec((1, ch), const)],
        out_specs=pl.BlockSpec((tb, ch), lambda i: (i, 0)),
        out_shape=jax.ShapeDtypeStruct((n, ch), F32),
        scratch_shapes=[pltpu.VMEM((tb + 2 * CONV_HALO, ch), F32)],
        compiler_params=_cparams(("parallel",)),
        name="conv",
    )(u, u, u, conv_w_l, conv_b_l, ng, nb_)


def _att_kernel(q_ref, k_ref, v_ref, o_ref):
    nheads = q_ref.shape[-1] // HEAD_DIM
    group = nheads // ATT_KV_HEADS
    for h in range(nheads):
        kv = h // group
        qh = q_ref[:, h * HEAD_DIM:(h + 1) * HEAD_DIM]
        kh = k_ref[0, :, kv * HEAD_DIM:(kv + 1) * HEAD_DIM]
        vh = v_ref[0, :, kv * HEAD_DIM:(kv + 1) * HEAD_DIM]
        s = lax.dot_general(qh, kh, (((1,), (1,)), ((), ())), preferred_element_type=F32)
        p = jnp.exp(s - jnp.max(s, axis=-1, keepdims=True))
        den = jnp.sum(p, axis=-1, keepdims=True)
        o = jnp.dot(p, vh, preferred_element_type=F32)
        o_ref[:, h * HEAD_DIM:(h + 1) * HEAD_DIM] = o / den


def _attention(q, keys, vals, nb, seq):
    n, w = q.shape
    ln = keys.shape[1]
    tq = 256
    nt = seq // tq
    return pl.pallas_call(
        _att_kernel,
        grid=(nb, nt),
        in_specs=[pl.BlockSpec((tq, w), lambda b, j: (b * nt + j, 0)),
                  pl.BlockSpec((1, ln, keys.shape[2]), lambda b, j: (b, 0, 0)),
                  pl.BlockSpec((1, ln, vals.shape[2]), lambda b, j: (b, 0, 0))],
        out_specs=pl.BlockSpec((tq, w), lambda b, j: (b * nt + j, 0)),
        out_shape=jax.ShapeDtypeStruct((n, w), F32),
        compiler_params=_cparams(("parallel", "parallel")),
        name="attention",
    )(q, keys, vals)


def _mid_kernel(x_ref, of_ref, ob_ref, g_ref, u_ref, att_ref, mod_ref, gng_ref, wo_ref, n2g_ref, wq_ref, keys_ref,
                xn_ref, h2_ref, sc_ref):
    gate1 = mod_ref[0, 2:3, :]
    shift2 = mod_ref[0, 3:4, :]
    scale2 = mod_ref[0, 4:5, :]
    og = of_ref[...] + ob_ref[...]
    w = og.shape[-1]
    o_gla = _head_rms(og, _seg_matrix(w, GLA_DK, 1.0 / GLA_DK), gng_ref[...]) * _silu(g_ref[...])
    mix = (jnp.dot(o_gla, wo_ref[0:256, :], preferred_element_type=F32)
           + jnp.dot(u_ref[...], wo_ref[256:512, :], preferred_element_type=F32)
           + jnp.dot(att_ref[...], wo_ref[512:1024, :], preferred_element_type=F32))
    xn = x_ref[...] + gate1 * mix
    xn_ref[...] = xn
    h2 = xn * lax.rsqrt(jnp.mean(xn * xn, axis=-1, keepdims=True) + EPS) * n2g_ref[...]
    h2 = h2 * (1.0 + scale2) + shift2
    h2_ref[...] = h2.astype(h2_ref.dtype)
    qry = jnp.dot(h2, wq_ref[...], preferred_element_type=F32)
    half = keys_ref.shape[-1]
    nkeys = keys_ref.shape[1]
    for hp in range(keys_ref.shape[0]):
        qs = qry[:, hp * half:(hp + 1) * half]
        sc_ref[hp * nkeys:(hp + 1) * nkeys, :] = lax.dot_general(
            keys_ref[hp], qs, (((1,), (1,)), ((), ())), preferred_element_type=F32)


def _mid(x, o_f, o_b, gla, u, att, mod, gng, wo, n2g, wq, keys2, tokens_per_mod):
    n, d = x.shape
    tm = 256
    tiles_per_mod = tokens_per_mod // tm
    nsub, nkeys, half = keys2.shape
    const = lambda i: (0, 0)
    row = lambda i: (i, 0)
    return pl.pallas_call(
        _mid_kernel,
        grid=(n // tm,),
        in_specs=[pl.BlockSpec((tm, d), row),
                  pl.BlockSpec((tm, 256), row), pl.BlockSpec((tm, 256), row),
                  pl.BlockSpec((tm, 256), lambda i: (i, 3)),
                  pl.BlockSpec((tm, 256), row), pl.BlockSpec((tm, 512), row),
                  pl.BlockSpec((1, ADA_CHUNKS, d), lambda i: (i // tiles_per_mod, 0, 0)),
                  pl.BlockSpec((1, 256), const),
                  pl.BlockSpec((d, d), const),
                  pl.BlockSpec((1, d), const),
                  pl.BlockSpec((d, wq.shape[1]), const),
                  pl.BlockSpec((nsub, nkeys, half), lambda i: (0, 0, 0))],
        out_specs=[pl.BlockSpec((tm, d), row), pl.BlockSpec((tm, d), row),
                   pl.BlockSpec((nsub * nkeys, tm), lambda i: (0, i))],
        out_shape=[jax.ShapeDtypeStruct((n, d), F32), jax.ShapeDtypeStruct((n, d), BF16),
                   jax.ShapeDtypeStruct((nsub * nkeys, n), F32)],
        compiler_params=_cparams(("parallel",)),
        name="mid",
    )(x, o_f, o_b, gla, u, att, mod, gng, wo, n2g, wq, keys2)


_PAIRS = [(a, b) for a in range(PEER_TOPK) for b in range(PEER_TOPK) if (a + 1) * (b + 1) <= PEER_TOPK]


def _extract_top(work, count):
    n = work.shape[0]
    iota = lax.broadcasted_iota(jnp.int32, work.shape, 0)
    rank = jnp.full(work.shape, count, jnp.int32)
    vals = []
    for r in range(count):
        m = jnp.max(work, axis=0, keepdims=True)
        first = jnp.min(jnp.where(work == m, iota, n), axis=0, keepdims=True)
        hit = iota == first
        rank = jnp.where(hit, r, rank)
        work = jnp.where(hit, -jnp.inf, work)
        vals.append(m)
    return rank, vals


def _topk_kernel(sc_ref, r2_ref, e2_ref, cnt_ref, cw_ref):
    nkeys = r2_ref.shape[1]
    for h in range(r2_ref.shape[0]):
        s1 = sc_ref[(2 * h) * nkeys:(2 * h + 1) * nkeys, :]
        s2 = sc_ref[(2 * h + 1) * nkeys:(2 * h + 2) * nkeys, :]
        rank1, v1 = _extract_top(s1, PEER_TOPK)
        rank2, v2 = _extract_top(s2, PEER_TOPK)
        npad = -len(_PAIRS) % 8
        fill = [jnp.full((npad, s1.shape[1]), -jnp.inf, F32)]
        cand = jnp.concatenate([v1[a] + v2[b] for a, b in _PAIRS] + fill, axis=0)
        pick, _ = _extract_top(cand, PEER_TOPK)
        picked = pick < PEER_TOPK
        e1 = [jnp.exp(v - v1[0]) for v in v1]
        e2 = [jnp.exp(v - v2[0]) for v in v2]
        gate = jnp.concatenate([e1[a] * e2[b] for a, b in _PAIRS] + [jnp.zeros_like(fill[0])], axis=0)
        z = jnp.sum(jnp.where(picked, gate, 0.0), axis=0, keepdims=True)
        cnt_dense = jnp.zeros(s1.shape, F32)
        row = 0
        for a in range(PEER_TOPK):
            nb = sum(1 for pa, _ in _PAIRS if pa == a)
            cnt_a = jnp.sum(jnp.where(picked[row:row + nb, :], 1.0, 0.0), axis=0, keepdims=True)
            cnt_dense = jnp.where(rank1 == a, cnt_a, cnt_dense)
            row += nb
        r2_ref[h] = rank2.astype(F32)
        e2_ref[h] = jnp.exp(s2 - v2[0])
        cnt_ref[h] = cnt_dense
        cw_ref[h] = jnp.exp(s1 - v1[0]) / z


def _peer_topk(scores, nheads):
    rows, n = scores.shape
    nkeys = rows // (2 * nheads)
    tk = 128
    spec = pl.BlockSpec((nheads, nkeys, tk), lambda i: (0, 0, i))
    shape = jax.ShapeDtypeStruct((nheads, nkeys, n), F32)
    return pl.pallas_call(
        _topk_kernel,
        grid=(n // tk,),
        in_specs=[pl.BlockSpec((rows, tk), lambda i: (0, i))],
        out_specs=[spec] * 4,
        out_shape=[shape] * 4,
        compiler_params=_cparams(("parallel",)),
        name="peer_topk",
    )(scores)


def _gelu(x):
    return 0.5 * x * (1.0 + lax.erf(x * (2.0 ** -0.5)))


def _peer_kernel(h2_ref, xn_ref, mod_ref, r2_ref, e2_ref, cnt_ref, cw_ref, u_ref, vt_ref, o_ref,
                 act_ref, w_ref, acc_ref):
    e = pl.program_id(1)
    nheads, nkeys, tb = r2_ref.shape
    te = u_ref.shape[0]
    rows_per_step = te // nkeys

    @pl.when(e == 0)
    def _():
        acc_ref[...] = jnp.zeros_like(acc_ref)

    act_ref[...] = lax.dot_general(u_ref[...], h2_ref[...], (((1,), (1,)), ((), ())), preferred_element_type=F32)

    def block(t, carry):
        grp = t // (tb // LANES)
        tc = t % (tb // LANES)
        lanes = pl.ds(pl.multiple_of(tc * LANES, LANES), LANES)
        rows8 = pl.ds(pl.multiple_of(e * rows_per_step + grp * SUBLANES, SUBLANES), SUBLANES)
        cnt8 = [cnt_ref[h, rows8, lanes] for h in range(nheads)]
        cw8 = [cw_ref[h, rows8, lanes] for h in range(nheads)]
        for r in range(SUBLANES):
            gsum = jnp.zeros((nkeys, LANES), F32)
            for h in range(nheads):
                sel = r2_ref[h, :, lanes] < cnt8[h][r:r + 1, :]
                gsum = gsum + jnp.where(sel, e2_ref[h, :, lanes], 0.0) * cw8[h][r:r + 1, :]
            erows = pl.ds(pl.multiple_of((grp * SUBLANES + r) * nkeys, nkeys), nkeys)
            w_ref[erows, lanes] = (gsum * _gelu(act_ref[erows, lanes])).astype(w_ref.dtype)
        return carry

    assert rows_per_step % SUBLANES == 0
    lax.fori_loop(0, (rows_per_step // SUBLANES) * (tb // LANES), block, 0)
    acc_ref[...] += jnp.dot(vt_ref[...], w_ref[...], preferred_element_type=F32)

    @pl.when(e == pl.num_programs(1) - 1)
    def _():
        gate2 = mod_ref[0, 5:6, :]
        o_ref[...] = xn_ref[...] + gate2 * acc_ref[...].T


def _peer_dense(h2, xn, mod, r2, e2, cnt, cw, u_bf, vt_bf, tokens_per_mod):
    n, d = xn.shape
    nheads, nkeys, _ = r2.shape
    nexp = u_bf.shape[0]
    tb = 512
    te = SUBLANES * nkeys
    blocks_per_mod = tokens_per_mod // tb
    tok = lambda i, e: (i, 0)
    aux = pl.BlockSpec((nheads, nkeys, tb), lambda i, e: (0, 0, i))
    return pl.pallas_call(
        _peer_kernel,
        grid=(n // tb, nexp // te),
        in_specs=[pl.BlockSpec((tb, d), tok), pl.BlockSpec((tb, d), tok),
                  pl.BlockSpec((1, ADA_CHUNKS, d), lambda i, e: (i // blocks_per_mod, 0, 0)),
                  aux, aux, aux, aux,
                  pl.BlockSpec((te, d), lambda i, e: (e, 0)),
                  pl.BlockSpec((d, te), lambda i, e: (0, e))],
        out_specs=pl.BlockSpec((tb, d), tok),
        out_shape=jax.ShapeDtypeStruct((n, d), F32),
        scratch_shapes=[pltpu.VMEM((te, tb), F32), pltpu.VMEM((te, tb), BF16), pltpu.VMEM((d, tb), F32)],
        compiler_params=_cparams(("parallel", "arbitrary")),
        name="peer_dense",
    )(h2, xn, mod, r2, e2, cnt, cw, u_bf, vt_bf)


def _rope_tables(seq, reps_q, reps_k):
    m = HEAD_DIM // 4
    freqs = ROPE_THETA ** (-jnp.arange(m, dtype=F32) / m)
    pos = jnp.arange(seq)
    ang_r = (pos // GRID_W).astype(F32)[:, None] * freqs[None, :]
    ang_c = (pos % GRID_W).astype(F32)[:, None] * freqs[None, :]
    cos = jnp.concatenate([jnp.cos(ang_r)] * 2 + [jnp.cos(ang_c)] * 2, axis=-1)
    sin = jnp.concatenate([-jnp.sin(ang_r), jnp.sin(ang_r), -jnp.sin(ang_c), jnp.sin(ang_c)], axis=-1)
    return (jnp.tile(cos, (1, reps_q)), jnp.tile(sin, (1, reps_q)),
            jnp.tile(cos, (1, reps_k)), jnp.tile(sin, (1, reps_k)))


def _layer_weights(p, l):
    d = p['w_in'].shape[1]
    w_in = p['w_in'][l]
    win_r = jnp.concatenate([w_in[:, 0:1024], w_in[:, 1056:2336], w_in[:, 1024:1056],
                             jnp.zeros((d, LANES - 2 * GLA_RANK), F32)], axis=1)
    aw = jnp.zeros((LANES, 512), F32)
    aw = aw.at[0:GLA_RANK, 0:256].set(p['gla_a_w_f'][l]).at[GLA_RANK:2 * GLA_RANK, 256:512].set(p['gla_a_w_b'][l])
    ab = jnp.concatenate([p['gla_a_b_f'][l], p['gla_a_b_b'][l]]).reshape(1, 512)
    nkeys, half = p['peer_keys'].shape[3], p['peer_keys'].shape[4]
    return dict(
        win_r=win_r, aw=aw, ab=ab,
        n1g=p['norm1_g'][l].reshape(1, d), n2g=p['norm2_g'][l].reshape(1, d),
        qg=jnp.tile(p['q_norm_g'][l], 8).reshape(1, 512), kg=jnp.tile(p['k_norm_g'][l], 2).reshape(1, 128),
        gng=p['gla_norm_g'][l].reshape(1, 256),
        conv_w=p['conv_w'][l], conv_b=p['conv_b'][l].reshape(1, 256),
        cng=p['conv_norm_g'][l].reshape(1, 256), cnb=p['conv_norm_b'][l].reshape(1, 256),
        wo=p['w_out'][l], wq=p['peer_wq'][l],
        keys2=p['peer_keys'][l].reshape(2 * PEER_HEADS, nkeys, half),
        u_bf=p['peer_u'][l].astype(BF16), vt_bf=p['peer_v'][l].astype(BF16).T,
    )


def _trunk_layer(x, mod, lw, nb, seq, tokens_per_mod, ctx, rope_tabs):
    gla, la, u, q, k, v = _pre(x, mod, lw['n1g'], lw['win_r'], lw['aw'], lw['ab'], lw['qg'], lw['kg'],
                               rope_tabs, tokens_per_mod, seq)
    if ctx is None:
        s0f = s0b = jnp.zeros((nb, GLA_HEADS, GLA_DK, GLA_DK), F32)
        keys = k.reshape(nb, seq, 128)
        vals = v.reshape(nb, seq, 128)
    else:
        ck, cv, sf, sb = ctx
        s0f = jnp.swapaxes(sf, -1, -2)
        s0b = jnp.swapaxes(sb, -1, -2)
        keys = jnp.concatenate([ck, k.reshape(nb, seq, 128)], axis=1)
        vals = jnp.concatenate([cv, v.reshape(nb, seq, 128)], axis=1)
    o_f, stf = _gla_scan(gla, la, s0f, nb, seq, False)
    o_b, stb = _gla_scan(gla, la, s0b, nb, seq, True)
    uc = _conv_branch(u, lw['conv_w'], lw['conv_b'], lw['cng'], lw['cnb'], seq)
    att = _attention(q, keys, vals, nb, seq)
    xn, h2, scores = _mid(x, o_f, o_b, gla, uc, att, mod, lw['gng'], lw['wo'], lw['n2g'], lw['wq'], lw['keys2'],
                          tokens_per_mod)
    r2, e2, cnt, cw = _peer_topk(scores, PEER_HEADS)
    x_out = _peer_dense(h2, xn, mod, r2, e2, cnt, cw, lw['u_bf'], lw['vt_bf'], tokens_per_mod)
    return x_out, k, v, jnp.swapaxes(stf, -1, -2), jnp.swapaxes(stb, -1, -2)


def kernel(x_prompt, x_sample, c, cache_k, cache_v, state_gla, c_ctx, ada_w, ada_b, norm1_g, norm2_g, w_in,
           gla_a_w_f, gla_a_b_f, gla_a_w_b, gla_a_b_b, gla_norm_g, conv_w, conv_b, conv_norm_g, conv_norm_b,
           q_norm_g, k_norm_g, w_out, peer_wq, peer_keys, peer_u, peer_v):
    p = dict(ada_w=ada_w, ada_b=ada_b, norm1_g=norm1_g, norm2_g=norm2_g, w_in=w_in, gla_a_w_f=gla_a_w_f,
             gla_a_b_f=gla_a_b_f, gla_a_w_b=gla_a_w_b, gla_a_b_b=gla_a_b_b, gla_norm_g=gla_norm_g, conv_w=conv_w,
             conv_b=conv_b, conv_norm_g=conv_norm_g, conv_norm_b=conv_norm_b, q_norm_g=q_norm_g, k_norm_g=k_norm_g,
             w_out=w_out, peer_wq=peer_wq, peer_keys=peer_keys, peer_u=peer_u, peer_v=peer_v)
    depth = ada_w.shape[0]
    bsz, seq, d = x_prompt.shape
    dbsz, dseq, _ = x_sample.shape
    assert dbsz + 1 <= 8
    cvecs = jnp.concatenate([c_ctx.reshape(1, d), c, jnp.zeros((8 - 1 - dbsz, d), F32)], axis=0)
    rope_tabs = _rope_tables(dseq, 512 // HEAD_DIM, 128 // HEAD_DIM)

    xc = x_prompt.reshape(bsz * seq, d)
    xs = x_sample.reshape(dbsz * dseq, d)
    ks_, vs_, sts_ = [], [], []
    for l in range(depth):
        lw = _layer_weights(p, l)
        mod = _modulation(cvecs, ada_w[l], ada_b[l]).reshape(8, ADA_CHUNKS, d)
        xc, k_l, v_l, sf_l, sb_l = _trunk_layer(xc, mod[0:1], lw, bsz, seq, bsz * seq, None, None)
        ks_.append(k_l.reshape(bsz, seq, ATT_KV_HEADS, HEAD_DIM))
        vs_.append(v_l.reshape(bsz, seq, ATT_KV_HEADS, HEAD_DIM))
        sts_.append(jnp.stack([sf_l, sb_l], axis=1))
        ctx = (cache_k[:, l].reshape(dbsz, -1, 128), cache_v[:, l].reshape(dbsz, -1, 128),
               state_gla[:, l, 0], state_gla[:, l, 1])
        xs, _, _, _, _ = _trunk_layer(xs, mod[1:1 + dbsz], lw, dbsz, dseq, dseq, ctx, rope_tabs)
    return (xc.reshape(bsz, seq, d), xs.reshape(dbsz, dseq, d),
            jnp.stack(ks_, axis=1), jnp.stack(vs_, axis=1), jnp.stack(sts_, axis=1))
```

```python
import functools

import jax
import jax.numpy as jnp
from jax import lax
from jax.experimental import pallas as pl
from jax.experimental.pallas import tpu as pltpu

F32 = jnp.float32
BF16 = jnp.bfloat16

EPS = 1e-6
ADA_CHUNKS = 6
GRID_W = 64
ROPE_THETA = 10000.0
GLA_HEADS = 4
GLA_DK = 64
GLA_RANK = 16
GLA_NORMALIZER = 16.0
GLA_CHUNK = 64
CONV_GROUPS = 4
CONV_K = 31
HEAD_DIM = 64
ATT_KV_HEADS = 2
PEER_HEADS = 8
PEER_TOPK = 16

LANES = 128
SUBLANES = 8
BF16_ROWS = 16
PEER_TC = 512
PEER_JC = 32
VMEM_LIMIT = 56 * 1024 * 1024


def _cparams(sem):
    return pltpu.CompilerParams(dimension_semantics=sem, vmem_limit_bytes=VMEM_LIMIT)


def _split_dot(x, m_bf16, terms):
    acc = None
    rem = x
    for _ in range(terms):
        piece = rem.astype(BF16)
        part = jnp.dot(piece, m_bf16, preferred_element_type=F32)
        acc = part if acc is None else acc + part
        rem = rem - piece.astype(F32)
    return acc


def _seg_matrix(width, seg, value):
    r = lax.broadcasted_iota(jnp.int32, (width, width), 0) // seg
    c = lax.broadcasted_iota(jnp.int32, (width, width), 1) // seg
    return jnp.where(r == c, value, 0.0).astype(BF16)


def _sigmoid(x):
    return 1.0 / (1.0 + jnp.exp(-x))


def _silu(x):
    return x * _sigmoid(x)


def _mod_kernel(c_ref, w_ref, b_ref, o_ref):
    c = c_ref[...]
    o_ref[...] = jnp.dot(_silu(c), w_ref[...], preferred_element_type=F32) + b_ref[...]


def _modulation(cvecs, ada_w_l, ada_b_l):
    d, n = ada_w_l.shape
    tn = 1024
    return pl.pallas_call(
        _mod_kernel,
        grid=(n // tn,),
        in_specs=[pl.BlockSpec((8, d), lambda j: (0, 0)),
                  pl.BlockSpec((d, tn), lambda j: (0, j)),
                  pl.BlockSpec((1, tn), lambda j: (0, j))],
        out_specs=pl.BlockSpec((8, tn), lambda j: (0, j)),
        out_shape=jax.ShapeDtypeStruct((8, n), F32),
        compiler_params=_cparams(("parallel",)),
        name="modulation",
    )(cvecs, ada_w_l, ada_b_l.reshape(1, n))


def _head_rms(x, seg_mat, gain):
    ms = _split_dot(x * x, seg_mat, 2)
    return x * lax.rsqrt(ms + EPS) * gain


def _rope(x, cos, sin_signed):
    w = x.shape[-1]
    lane = lax.broadcasted_iota(jnp.int32, x.shape, 1)
    partner = jnp.where((lane % 32) < 16, pltpu.roll(x, w - 16, 1), pltpu.roll(x, 16, 1))
    return x * cos + partner * sin_signed


def _pre_kernel(rope, x_ref, mod_ref, n1g_ref, win_ref, aw_ref, ab_ref, qg_ref, kg_ref, *rest):
    if rope:
        cq_ref, sq_ref, ck_ref, sk_ref, gla_ref, la_ref, u_ref, q_ref, k_ref, v_ref = rest
    else:
        gla_ref, la_ref, u_ref, q_ref, k_ref, v_ref = rest
    x = x_ref[...]
    shift = mod_ref[0, 0:1, :]
    scale = mod_ref[0, 1:2, :]
    h = x * lax.rsqrt(jnp.mean(x * x, axis=-1, keepdims=True) + EPS) * n1g_ref[...]
    h = h * (1.0 + scale) + shift
    proj = jnp.dot(h, win_ref[...], preferred_element_type=F32)
    gla_ref[:, 0:256] = proj[:, 0:256] * (GLA_DK ** -0.5)
    gla_ref[:, 256:1024] = proj[:, 256:1024]
    u_ref[...] = proj[:, 1024:1280] * _sigmoid(proj[:, 1280:1536])
    la_pre = jnp.dot(proj[:, 2304:2432], aw_ref[...], preferred_element_type=F32) + ab_ref[...]
    la_ref[...] = (jnp.minimum(la_pre, 0.0) - jnp.log1p(jnp.exp(-jnp.abs(la_pre)))) * (1.0 / GLA_NORMALIZER)
    q = _head_rms(proj[:, 1536:2048], _seg_matrix(512, HEAD_DIM, 1.0 / HEAD_DIM), qg_ref[...])
    k = _head_rms(proj[:, 2048:2176], _seg_matrix(128, HEAD_DIM, 1.0 / HEAD_DIM), kg_ref[...])
    if rope:
        q = _rope(q, cq_ref[...], sq_ref[...])
        k = _rope(k, ck_ref[...], sk_ref[...])
    q_ref[...] = q * (HEAD_DIM ** -0.5)
    k_ref[...] = k
    v_ref[...] = proj[:, 2176:2304]


def _pre(x, mod, n1g, win_r, aw, ab, qg, kg, rope_tabs, tokens_per_mod, seq):
    n, d = x.shape
    tm = 256
    cols = win_r.shape[1]
    rope = rope_tabs is not None
    tiles_per_mod = tokens_per_mod // tm
    tiles_per_seq = seq // tm
    const = lambda i: (0, 0)
    row = lambda i: (i, 0)
    in_specs = [pl.BlockSpec((tm, d), row),
                pl.BlockSpec((1, ADA_CHUNKS, d), lambda i: (i // tiles_per_mod, 0, 0)),
                pl.BlockSpec((1, d), const),
                pl.BlockSpec((d, cols), const),
                pl.BlockSpec((LANES, 512), const),
                pl.BlockSpec((1, 512), const),
                pl.BlockSpec((1, 512), const),
                pl.BlockSpec((1, 128), const)]
    args = [x, mod, n1g, win_r, aw, ab, qg, kg]
    if rope:
        pos = lambda i: (i % tiles_per_seq, 0)
        in_specs += [pl.BlockSpec((tm, 512), pos), pl.BlockSpec((tm, 512), pos),
                     pl.BlockSpec((tm, 128), pos), pl.BlockSpec((tm, 128), pos)]
        args += list(rope_tabs)
    widths = (1024, 512, 256, 512, 128, 128)
    return pl.pallas_call(
        functools.partial(_pre_kernel, rope),
        grid=(n // tm,),
        in_specs=in_specs,
        out_specs=[pl.BlockSpec((tm, w), row) for w in widths],
        out_shape=[jax.ShapeDtypeStruct((n, w), F32) for w in widths],
        compiler_params=_cparams(("parallel",)),
        name="pre_rope" if rope else "pre",
    )(*args)


def _gla_kernel(reverse, nchunk, q_ref, k_ref, v_ref, la_ref, s0_ref, o_ref, sT_out_ref, st_ref):
    j = pl.program_id(1)

    @pl.when(j == 0)
    def _():
        st_ref[...] = s0_ref[0]

    r = lax.broadcasted_iota(jnp.int32, (GLA_CHUNK, GLA_CHUNK), 0)
    c = lax.broadcasted_iota(jnp.int32, (GLA_CHUNK, GLA_CHUNK), 1)
    keep = (c >= r) if reverse else (c <= r)
    tri = jnp.where(keep, 1.0, 0.0).astype(BF16)
    order = range(nchunk - 1, -1, -1) if reverse else range(nchunk)
    for ci in order:
        rows = slice(ci * GLA_CHUNK, (ci + 1) * GLA_CHUNK)
        la = la_ref[rows, :]
        bcum = None
        rem = la
        for _ in range(3):
            piece = rem.astype(BF16)
            part = jnp.dot(tri, piece, preferred_element_type=F32)
            bcum = part if bcum is None else bcum + part
            rem = rem - piece.astype(F32)
        for h in range(GLA_HEADS):
            cs = slice(h * GLA_DK, (h + 1) * GLA_DK)
            qh = q_ref[rows, cs]
            kh = k_ref[rows, cs]
            vh = v_ref[rows, cs]
            bh = bcum[:, cs]
            b_end = bh[0:1, :] if reverse else bh[GLA_CHUNK - 1:GLA_CHUNK, :]
            q_dec = qh * jnp.exp(bh)
            k_dec = kh * jnp.exp(-bh)
            a = lax.dot_general(q_dec, k_dec, (((1,), (1,)), ((), ())), preferred_element_type=F32)
            a = jnp.where(keep, a, 0.0)
            st = st_ref[h]
            o = jnp.dot(a, vh, preferred_element_type=F32) + lax.dot_general(
                q_dec, st, (((1,), (1,)), ((), ())), preferred_element_type=F32)
            o_ref[rows, cs] = o
            k_tail = kh * jnp.exp(b_end - bh)
            st_ref[h] = st * jnp.exp(b_end) + lax.dot_general(
                vh, k_tail, (((0,), (0,)), ((), ())), preferred_element_type=F32)

    @pl.when(j == pl.num_programs(1) - 1)
    def _():
        sT_out_ref[0] = st_ref[...]


def _gla_scan(gla, la, s0T, nb, seq, reverse):
    n = gla.shape[0]
    tb = 256
    nt = seq // tb
    w = GLA_HEADS * GLA_DK

    def rowblk(b, j):
        return b * nt + (nt - 1 - j if reverse else j)

    la_col = 1 if reverse else 0
    st_shape = (GLA_HEADS, GLA_DK, GLA_DK)
    return pl.pallas_call(
        functools.partial(_gla_kernel, reverse, tb // GLA_CHUNK),
        grid=(nb, nt),
        in_specs=[pl.BlockSpec((tb, w), lambda b, j: (rowblk(b, j), 0)),
                  pl.BlockSpec((tb, w), lambda b, j: (rowblk(b, j), 1)),
                  pl.BlockSpec((tb, w), lambda b, j: (rowblk(b, j), 2)),
                  pl.BlockSpec((tb, w), lambda b, j: (rowblk(b, j), la_col)),
                  pl.BlockSpec((1,) + st_shape, lambda b, j: (b, 0, 0, 0))],
        out_specs=[pl.BlockSpec((tb, w), lambda b, j: (rowblk(b, j), 0)),
                   pl.BlockSpec((1,) + st_shape, lambda b, j: (b, 0, 0, 0))],
        out_shape=[jax.ShapeDtypeStruct((n, w), F32),
                   jax.ShapeDtypeStruct((nb,) + st_shape, F32)],
        scratch_shapes=[pltpu.VMEM(st_shape, F32)],
        compiler_params=_cparams(("parallel", "arbitrary")),
        name="gla_bwd" if reverse else "gla_fwd",
    )(gla, gla, gla, la, s0T)


CONV_HALO = 16


def _conv_kernel(nt, prev_ref, cur_ref, next_ref, w_ref, b_ref, ng_ref, nb_ref, o_ref, pad_ref):
    j = pl.program_id(0) % nt
    tb = cur_ref.shape[0]
    pad_ref[0:CONV_HALO, :] = jnp.where(j > 0, prev_ref[...], 0.0)
    pad_ref[CONV_HALO:CONV_HALO + tb, :] = cur_ref[...]
    pad_ref[CONV_HALO + tb:, :] = jnp.where(j < nt - 1, next_ref[...], 0.0)
    base = CONV_HALO - CONV_K // 2
    acc = None
    for kk in range(CONV_K):
        term = pad_ref[base + kk:base + kk + tb, :] * w_ref[kk:kk + 1, :]
        acc = term if acc is None else acc + term
    y = acc + b_ref[...]
    ch = y.shape[-1]
    seg = _seg_matrix(ch, ch // CONV_GROUPS, float(CONV_GROUPS) / ch)
    dlt = y - _split_dot(y, seg, 3)
    var = _split_dot(dlt * dlt, seg, 2)
    z = dlt * lax.rsqrt(var + EPS) * ng_ref[...] + nb_ref[...]
    o_ref[...] = _silu(z)


def _conv_branch(u, conv_w_l, conv_b_l, ng, nb_, seq):
    n, ch = u.shape
    tb = 256
    nt = seq // tb
    hb = tb // CONV_HALO
    nhalo = n // CONV_HALO
    const = lambda i: (0, 0)
    return pl.pallas_call(
        functools.partial(_conv_kernel, nt),
        grid=(n // tb,),
        in_specs=[pl.BlockSpec((CONV_HALO, ch), lambda i: (jnp.maximum(i * hb - 1, 0), 0)),
                  pl.BlockSpec((tb, ch), lambda i: (i, 0)),
                  pl.BlockSpec((CONV_HALO, ch), lambda i: (jnp.minimum((i + 1) * hb, nhalo - 1), 0)),
                  pl.BlockSpec((CONV_K, ch), const),
                  pl.BlockSpec((1, ch), const), pl.BlockSpec((1, ch), const), pl.BlockSpec((1, ch), const)],
        out_specs=pl.BlockSpec((tb, ch), lambda i: (i, 0)),
        out_shape=jax.ShapeDtypeStruct((n, ch), F32),
        scratch_shapes=[pltpu.VMEM((tb + 2 * CONV_HALO, ch), F32)],
        compiler_params=_cparams(("parallel",)),
        name="conv",
    )(u, u, u, conv_w_l, conv_b_l, ng, nb_)


def _att_kernel(q_ref, k_ref, v_ref, o_ref):
    nheads = q_ref.shape[-1] // HEAD_DIM
    group = nheads // ATT_KV_HEADS
    for h in range(nheads):
        kv = h // group
        qh = q_ref[:, h * HEAD_DIM:(h + 1) * HEAD_DIM]
        kh = k_ref[0, :, kv * HEAD_DIM:(kv + 1) * HEAD_DIM]
        vh = v_ref[0, :, kv * HEAD_DIM:(kv + 1) * HEAD_DIM]
        s = lax.dot_general(qh, kh, (((1,), (1,)), ((), ())), preferred_element_type=F32)
        p = jnp.exp(s - jnp.max(s, axis=-1, keepdims=True))
        den = jnp.sum(p, axis=-1, keepdims=True)
        o = jnp.dot(p, vh, preferred_element_type=F32)
        o_ref[:, h * HEAD_DIM:(h + 1) * HEAD_DIM] = o / den


def _attention(q, keys, vals, nb, seq):
    n, w = q.shape
    ln = keys.shape[1]
    tq = 256
    nt = seq // tq
    return pl.pallas_call(
        _att_kernel,
        grid=(nb, nt),
        in_specs=[pl.BlockSpec((tq, w), lambda b, j: (b * nt + j, 0)),
                  pl.BlockSpec((1, ln, keys.shape[2]), lambda b, j: (b, 0, 0)),
                  pl.BlockSpec((1, ln, vals.shape[2]), lambda b, j: (b, 0, 0))],
        out_specs=pl.BlockSpec((tq, w), lambda b, j: (b * nt + j, 0)),
        out_shape=jax.ShapeDtypeStruct((n, w), F32),
        compiler_params=_cparams(("parallel", "parallel")),
        name="attention",
    )(q, keys, vals)


def _mid_kernel(x_ref, of_ref, ob_ref, g_ref, u_ref, att_ref, mod_ref, gng_ref, wo_ref, n2g_ref, wq_ref, keys_ref,
                xn_ref, h2t_ref, sc_ref):
    gate1 = mod_ref[0, 2:3, :]
    shift2 = mod_ref[0, 3:4, :]
    scale2 = mod_ref[0, 4:5, :]
    og = of_ref[...] + ob_ref[...]
    w = og.shape[-1]
    o_gla = _head_rms(og, _seg_matrix(w, GLA_DK, 1.0 / GLA_DK), gng_ref[...]) * _silu(g_ref[...])
    mix = (jnp.dot(o_gla, wo_ref[0:256, :], preferred_element_type=F32)
           + jnp.dot(u_ref[...], wo_ref[256:512, :], preferred_element_type=F32)
           + jnp.dot(att_ref[...], wo_ref[512:1024, :], preferred_element_type=F32))
    xn = x_ref[...] + gate1 * mix
    xn_ref[...] = xn
    h2 = xn * lax.rsqrt(jnp.mean(xn * xn, axis=-1, keepdims=True) + EPS) * n2g_ref[...]
    h2 = h2 * (1.0 + scale2) + shift2
    h2t_ref[...] = h2.T.astype(h2t_ref.dtype)
    qry = jnp.dot(h2, wq_ref[...], preferred_element_type=F32)
    half = keys_ref.shape[-1]
    nkeys = keys_ref.shape[1]
    for hp in range(keys_ref.shape[0]):
        qs = qry[:, hp * half:(hp + 1) * half]
        sc_ref[hp * nkeys:(hp + 1) * nkeys, :] = lax.dot_general(
            keys_ref[hp], qs, (((1,), (1,)), ((), ())), preferred_element_type=F32)


def _mid(x, o_f, o_b, gla, u, att, mod, gng, wo, n2g, wq, keys2, tokens_per_mod):
    n, d = x.shape
    tm = 256
    tiles_per_mod = tokens_per_mod // tm
    nsub, nkeys, half = keys2.shape
    const = lambda i: (0, 0)
    row = lambda i: (i, 0)
    return pl.pallas_call(
        _mid_kernel,
        grid=(n // tm,),
        in_specs=[pl.BlockSpec((tm, d), row),
                  pl.BlockSpec((tm, 256), row), pl.BlockSpec((tm, 256), row),
                  pl.BlockSpec((tm, 256), lambda i: (i, 3)),
                  pl.BlockSpec((tm, 256), row), pl.BlockSpec((tm, 512), row),
                  pl.BlockSpec((1, ADA_CHUNKS, d), lambda i: (i // tiles_per_mod, 0, 0)),
                  pl.BlockSpec((1, 256), const),
                  pl.BlockSpec((d, d), const),
                  pl.BlockSpec((1, d), const),
                  pl.BlockSpec((d, wq.shape[1]), const),
                  pl.BlockSpec((nsub, nkeys, half), lambda i: (0, 0, 0))],
        out_specs=[pl.BlockSpec((tm, d), row), pl.BlockSpec((d, tm), lambda i: (0, i)),
                   pl.BlockSpec((nsub * nkeys, tm), lambda i: (0, i))],
        out_shape=[jax.ShapeDtypeStruct((n, d), F32), jax.ShapeDtypeStruct((d, n), BF16),
                   jax.ShapeDtypeStruct((nsub * nkeys, n), F32)],
        compiler_params=_cparams(("parallel",)),
        name="mid",
    )(x, o_f, o_b, gla, u, att, mod, gng, wo, n2g, wq, keys2)


_PAIRS = [(a, b) for a in range(PEER_TOPK) for b in range(PEER_TOPK) if (a + 1) * (b + 1) <= PEER_TOPK]


def _extract_top(work, count):
    n = work.shape[0]
    iota = lax.broadcasted_iota(jnp.int32, work.shape, 0)
    rank = jnp.full(work.shape, count, jnp.int32)
    vals = []
    for r in range(count):
        m = jnp.max(work, axis=0, keepdims=True)
        first = jnp.min(jnp.where(work == m, iota, n), axis=0, keepdims=True)
        hit = iota == first
        rank = jnp.where(hit, r, rank)
        work = jnp.where(hit, -jnp.inf, work)
        vals.append(m)
    return rank, vals


def _topk_kernel(sc_ref, r2_ref, e2_ref, cnt_ref, cw_ref):
    nkeys = r2_ref.shape[1]
    for h in range(r2_ref.shape[0]):
        s1 = sc_ref[(2 * h) * nkeys:(2 * h + 1) * nkeys, :]
        s2 = sc_ref[(2 * h + 1) * nkeys:(2 * h + 2) * nkeys, :]
        rank1, v1 = _extract_top(s1, PEER_TOPK)
        rank2, v2 = _extract_top(s2, PEER_TOPK)
        npad = -len(_PAIRS) % 8
        fill = [jnp.full((npad, s1.shape[1]), -jnp.inf, F32)]
        cand = jnp.concatenate([v1[a] + v2[b] for a, b in _PAIRS] + fill, axis=0)
        pick, _ = _extract_top(cand, PEER_TOPK)
        picked = pick < PEER_TOPK
        e1 = [jnp.exp(v - v1[0]) for v in v1]
        e2 = [jnp.exp(v - v2[0]) for v in v2]
        gate = jnp.concatenate([e1[a] * e2[b] for a, b in _PAIRS] + [jnp.zeros_like(fill[0])], axis=0)
        z = jnp.sum(jnp.where(picked, gate, 0.0), axis=0, keepdims=True)
        cnt_dense = jnp.zeros(s1.shape, F32)
        row = 0
        for a in range(PEER_TOPK):
            nb = sum(1 for pa, _ in _PAIRS if pa == a)
            cnt_a = jnp.sum(jnp.where(picked[row:row + nb, :], 1.0, 0.0), axis=0, keepdims=True)
            cnt_dense = jnp.where(rank1 == a, cnt_a, cnt_dense)
            row += nb
        r2_ref[h] = rank2.astype(F32).astype(r2_ref.dtype)
        e2_ref[h] = jnp.exp(s2 - v2[0]).astype(e2_ref.dtype)
        cnt_ref[h] = cnt_dense
        cw_ref[h] = jnp.exp(s1 - v1[0]) / z


def _peer_topk(scores, nheads):
    rows, n = scores.shape
    nkeys = rows // (2 * nheads)
    tk = 128
    spec = pl.BlockSpec((nheads, nkeys, tk), lambda i: (0, 0, i))
    shape = jax.ShapeDtypeStruct((nheads, nkeys, n), F32)
    return pl.pallas_call(
        _topk_kernel,
        grid=(n // tk,),
        in_specs=[pl.BlockSpec((rows, tk), lambda i: (0, i))],
        out_specs=[spec] * 4,
        out_shape=[jax.ShapeDtypeStruct(shape.shape, BF16)] * 2 + [shape] * 2,
        compiler_params=_cparams(("parallel",)),
        name="peer_topk",
    )(scores)


def _gelu(x):
    return 0.5 * x * (1.0 + lax.erf(x * (2.0 ** -0.5)))


def _peer_kernel(h2t_ref, xn_ref, mod_ref, r2_ref, e2_ref, cnt_ref, cw_ref, u_ref, vt_ref, o_ref,
                 rowb_ref, w_ref, acc_ref):
    e = pl.program_id(1)
    nheads, nkeys, tb = r2_ref.shape
    te = u_ref.shape[0]
    assert te == SUBLANES * nkeys and nkeys % PEER_JC == 0
    reps = PEER_JC // BF16_ROWS

    @pl.when(e == 0)
    def _():
        acc_ref[...] = jnp.zeros_like(acc_ref)

    act = jnp.dot(u_ref[...], h2t_ref[...], preferred_element_type=F32)
    rows8 = pl.ds(pl.multiple_of(e * SUBLANES, SUBLANES), SUBLANES)
    for h in range(nheads):
        cnt8 = cnt_ref[h, rows8, :]
        cw8 = cw_ref[h, rows8, :]
        for r in range(SUBLANES):
            rowb_ref[0, h * SUBLANES + r] = jnp.broadcast_to(cnt8[r:r + 1, :], (BF16_ROWS, tb)).astype(BF16)
            rowb_ref[1, h * SUBLANES + r] = jnp.broadcast_to(cw8[r:r + 1, :], (BF16_ROWS, tb)).astype(BF16)
    for jc in range(nkeys // PEER_JC):
        js = slice(jc * PEER_JC, (jc + 1) * PEER_JC)
        gs = [None] * SUBLANES
        for h in range(nheads):
            r2c = r2_ref[h, js, :]
            e2c = e2_ref[h, js, :]
            for r in range(SUBLANES):
                cnt_t = jnp.concatenate([rowb_ref[0, h * SUBLANES + r]] * reps, axis=0)
                cw_t = jnp.concatenate([rowb_ref[1, h * SUBLANES + r]] * reps, axis=0)
                term = jnp.where(r2c < cnt_t, e2c, jnp.zeros_like(e2c)) * cw_t
                gs[r] = term if gs[r] is None else gs[r] + term
        for r in range(SUBLANES):
            rows = slice(r * nkeys + jc * PEER_JC, r * nkeys + (jc + 1) * PEER_JC)
            w_ref[rows, :] = gs[r] * _gelu(act[rows, :]).astype(BF16)
    acc_ref[...] += jnp.dot(vt_ref[...], w_ref[...], preferred_element_type=F32)

    @pl.when(e == pl.num_programs(1) - 1)
    def _():
        gate2 = mod_ref[0, 5:6, :]
        o_ref[...] = xn_ref[...] + gate2 * acc_ref[...].T


def _peer_dense(h2t, xn, mod, r2, e2, cnt, cw, u_bf, vt_bf, tokens_per_mod):
    n, d = xn.shape
    nheads, nkeys, _ = r2.shape
    nexp = u_bf.shape[0]
    tb = PEER_TC
    te = SUBLANES * nkeys
    blocks_per_mod = tokens_per_mod // tb
    tok = lambda i, e: (i, 0)
    aux = pl.BlockSpec((nheads, nkeys, tb), lambda i, e: (0, 0, i))
    return pl.pallas_call(
        _peer_kernel,
        grid=(n // tb, nexp // te),
        in_specs=[pl.BlockSpec((d, tb), lambda i, e: (0, i)), pl.BlockSpec((tb, d), tok),
                  pl.BlockSpec((1, ADA_CHUNKS, d), lambda i, e: (i // blocks_per_mod, 0, 0)),
                  aux, aux, aux, aux,
                  pl.BlockSpec((te, d), lambda i, e: (e, 0)),
                  pl.BlockSpec((d, te), lambda i, e: (0, e))],
        out_specs=pl.BlockSpec((tb, d), tok),
        out_shape=jax.ShapeDtypeStruct((n, d), F32),
        scratch_shapes=[pltpu.VMEM((2, nheads * SUBLANES, BF16_ROWS, tb), BF16),
                        pltpu.VMEM((te, tb), BF16), pltpu.VMEM((d, tb), F32)],
        compiler_params=_cparams(("parallel", "arbitrary")),
        name="peer_dense",
    )(h2t, xn, mod, r2, e2, cnt, cw, u_bf, vt_bf)


def _rope_tables(seq, reps_q, reps_k):
    m = HEAD_DIM // 4
    freqs = ROPE_THETA ** (-jnp.arange(m, dtype=F32) / m)
    pos = jnp.arange(seq)
    ang_r = (pos // GRID_W).astype(F32)[:, None] * freqs[None, :]
    ang_c = (pos % GRID_W).astype(F32)[:, None] * freqs[None, :]
    cos = jnp.concatenate([jnp.cos(ang_r)] * 2 + [jnp.cos(ang_c)] * 2, axis=-1)
    sin = jnp.concatenate([-jnp.sin(ang_r), jnp.sin(ang_r), -jnp.sin(ang_c), jnp.sin(ang_c)], axis=-1)
    return (jnp.tile(cos, (1, reps_q)), jnp.tile(sin, (1, reps_q)),
            jnp.tile(cos, (1, reps_k)), jnp.tile(sin, (1, reps_k)))


def _layer_weights(p, l):
    d = p['w_in'].shape[1]
    w_in = p['w_in'][l]
    win_r = jnp.concatenate([w_in[:, 0:1024], w_in[:, 1056:2336], w_in[:, 1024:1056],
                             jnp.zeros((d, LANES - 2 * GLA_RANK), F32)], axis=1)
    aw = jnp.zeros((LANES, 512), F32)
    aw = aw.at[0:GLA_RANK, 0:256].set(p['gla_a_w_f'][l]).at[GLA_RANK:2 * GLA_RANK, 256:512].set(p['gla_a_w_b'][l])
    ab = jnp.concatenate([p['gla_a_b_f'][l], p['gla_a_b_b'][l]]).reshape(1, 512)
    nkeys, half = p['peer_keys'].shape[3], p['peer_keys'].shape[4]
    return dict(
        win_r=win_r, aw=aw, ab=ab,
        n1g=p['norm1_g'][l].reshape(1, d), n2g=p['norm2_g'][l].reshape(1, d),
        qg=jnp.tile(p['q_norm_g'][l], 8).reshape(1, 512), kg=jnp.tile(p['k_norm_g'][l], 2).reshape(1, 128),
        gng=p['gla_norm_g'][l].reshape(1, 256),
        conv_w=p['conv_w'][l], conv_b=p['conv_b'][l].reshape(1, 256),
        cng=p['conv_norm_g'][l].reshape(1, 256), cnb=p['conv_norm_b'][l].reshape(1, 256),
        wo=p['w_out'][l], wq=p['peer_wq'][l],
        keys2=p['peer_keys'][l].reshape(2 * PEER_HEADS, nkeys, half),
        u_bf=p['peer_u'][l].astype(BF16), vt_bf=p['peer_v'][l].astype(BF16).T,
    )


def _trunk_layer(x, mod, lw, nb, seq, tokens_per_mod, ctx, rope_tabs):
    gla, la, u, q, k, v = _pre(x, mod, lw['n1g'], lw['win_r'], lw['aw'], lw['ab'], lw['qg'], lw['kg'],
                               rope_tabs, tokens_per_mod, seq)
    if ctx is None:
        s0f = s0b = jnp.zeros((nb, GLA_HEADS, GLA_DK, GLA_DK), F32)
        keys = k.reshape(nb, seq, 128)
        vals = v.reshape(nb, seq, 128)
    else:
        ck, cv, sf, sb = ctx
        s0f = jnp.swapaxes(sf, -1, -2)
        s0b = jnp.swapaxes(sb, -1, -2)
        keys = jnp.concatenate([ck, k.reshape(nb, seq, 128)], axis=1)
        vals = jnp.concatenate([cv, v.reshape(nb, seq, 128)], axis=1)
    o_f, stf = _gla_scan(gla, la, s0f, nb, seq, False)
    o_b, stb = _gla_scan(gla, la, s0b, nb, seq, True)
    uc = _conv_branch(u, lw['conv_w'], lw['conv_b'], lw['cng'], lw['cnb'], seq)
    att = _attention(q, keys, vals, nb, seq)
    xn, h2t, scores = _mid(x, o_f, o_b, gla, uc, att, mod, lw['gng'], lw['wo'], lw['n2g'], lw['wq'], lw['keys2'],
                          tokens_per_mod)
    r2, e2, cnt, cw = _peer_topk(scores, PEER_HEADS)
    x_out = _peer_dense(h2t, xn, mod, r2, e2, cnt, cw, lw['u_bf'], lw['vt_bf'], tokens_per_mod)
    return x_out, k, v, jnp.swapaxes(stf, -1, -2), jnp.swapaxes(stb, -1, -2)


def kernel(x_prompt, x_sample, c, cache_k, cache_v, state_gla, c_ctx, ada_w, ada_b, norm1_g, norm2_g, w_in,
           gla_a_w_f, gla_a_b_f, gla_a_w_b, gla_a_b_b, gla_norm_g, conv_w, conv_b, conv_norm_g, conv_norm_b,
           q_norm_g, k_norm_g, w_out, peer_wq, peer_keys, peer_u, peer_v):
    p = dict(ada_w=ada_w, ada_b=ada_b, norm1_g=norm1_g, norm2_g=norm2_g, w_in=w_in, gla_a_w_f=gla_a_w_f,
             gla_a_b_f=gla_a_b_f, gla_a_w_b=gla_a_w_b, gla_a_b_b=gla_a_b_b, gla_norm_g=gla_norm_g, conv_w=conv_w,
             conv_b=conv_b, conv_norm_g=conv_norm_g, conv_norm_b=conv_norm_b, q_norm_g=q_norm_g, k_norm_g=k_norm_g,
             w_out=w_out, peer_wq=peer_wq, peer_keys=peer_keys, peer_u=peer_u, peer_v=peer_v)
    depth = ada_w.shape[0]
    bsz, seq, d = x_prompt.shape
    dbsz, dseq, _ = x_sample.shape
    assert dbsz + 1 <= 8
    cvecs = jnp.concatenate([c_ctx.reshape(1, d), c, jnp.zeros((8 - 1 - dbsz, d), F32)], axis=0)
    rope_tabs = _rope_tables(dseq, 512 // HEAD_DIM, 128 // HEAD_DIM)

    xc = x_prompt.reshape(bsz * seq, d)
    xs = x_sample.reshape(dbsz * dseq, d)
    ks_, vs_, sts_ = [], [], []
    for l in range(depth):
        lw = _layer_weights(p, l)
        mod = _modulation(cvecs, ada_w[l], ada_b[l]).reshape(8, ADA_CHUNKS, d)
        xc, k_l, v_l, sf_l, sb_l = _trunk_layer(xc, mod[0:1], lw, bsz, seq, bsz * seq, None, None)
        ks_.append(k_l.reshape(bsz, seq, ATT_KV_HEADS, HEAD_DIM))
        vs_.append(v_l.reshape(bsz, seq, ATT_KV_HEADS, HEAD_DIM))
        sts_.append(jnp.stack([sf_l, sb_l], axis=1))
        ctx = (cache_k[:, l].reshape(dbsz, -1, 128), cache_v[:, l].reshape(dbsz, -1, 128),
               state_gla[:, l, 0], state_gla[:, l, 1])
        xs, _, _, _, _ = _trunk_layer(xs, mod[1:1 + dbsz], lw, dbsz, dseq, dseq, ctx, rope_tabs)
    return (xc.reshape(bsz, seq, d), xs.reshape(dbsz, dseq, d),
            jnp.stack(ks_, axis=1), jnp.stack(vs_, axis=1), jnp.stack(sts_, axis=1))
```
